```python
import jax, jax.numpy as jnp
from jax import lax
import numpy as np

D_MODEL = 1024
BATCH = 16
SEQ = 2048
DEPTH = 1
DEC_BATCH = 8
DEC_SEQ = 32
PAST_LEN = 1024

CHUNK = 64
BAND_CHUNKS = 8
ATTN_PAST = BAND_CHUNKS * CHUNK
D_MIX = D_MODEL
SSD_WIDTH = D_MIX // 2
SSD_HEAD_DIM = 64
N_SSD_HEADS = SSD_WIDTH // SSD_HEAD_DIM
N_SSD_GROUPS = 2
HEADS_PER_GROUP = N_SSD_HEADS // N_SSD_GROUPS
D_STATE = 128
D_CONV = 4
CONV_DIM = SSD_WIDTH + 2 * N_SSD_GROUPS * D_STATE
ATTN_WIDTH = D_MIX - SSD_WIDTH
ATTN_HEAD_DIM = 64
N_ATTN_HEADS = ATTN_WIDTH // ATTN_HEAD_DIM
MAX_REL = 128
D_FF = -(-8 * D_MODEL // (3 * 256)) * 256
IN_SIZES = [SSD_WIDTH, CONV_DIM, N_SSD_HEADS, ATTN_WIDTH, ATTN_WIDTH, ATTN_WIDTH]
IN_PROJ = sum(IN_SIZES)
IN_SPLITS = [int(s) for s in np.cumsum(IN_SIZES)[:-1]]
EPS = 1e-6
NEG = -1e30

kernel_name = "hymba_ssd_chunkband_stream_step"


def rmsnorm(x, g):
    xf = x.astype(jnp.float32)
    y = xf * lax.rsqrt(jnp.mean(xf * xf, axis=-1, keepdims=True) + EPS)
    return (y * g.astype(jnp.float32)).astype(x.dtype)


def causal_conv(xbc, conv_state, w, b):
    l = xbc.shape[1]
    xpad = jnp.concatenate([conv_state.astype(xbc.dtype), xbc], axis=1)
    out = b
    for tap in range(D_CONV):
        out = out + xpad[:, tap:tap + l] * w[tap]
    return out, xpad[:, -(D_CONV - 1):]


def ssd_scan(x, dt, a, bm, cm, h0, chunk_len):
    bsz, l, nh, p = x.shape
    nc = l // chunk_len
    r = lambda t: t.reshape((bsz, nc, chunk_len) + t.shape[2:])
    x, dt, bm, cm = r(x), r(dt), r(bm), r(cm)
    da_cum = jnp.cumsum(dt * a, axis=2)
    seg = da_cum[:, :, :, None, :] - da_cum[:, :, None, :, :]
    causal = jnp.tril(jnp.ones((chunk_len, chunk_len), bool))[None, None, :, :, None]
    decay = jnp.where(causal, jnp.exp(jnp.where(causal, seg, 0.0)), 0.0)
    xdt = x * dt[..., None]
    scores = jnp.einsum('bcqhn,bcshn->bcqsh', cm, bm) * decay
    y_diag = jnp.einsum('bcqsh,bcshp->bcqhp', scores, xdt)
    decay_to_end = jnp.exp(da_cum[:, :, -1:, :] - da_cum)
    chunk_states = jnp.einsum('bcqhn,bcqh,bcqhp->bchpn', bm, decay_to_end, xdt)
    chunk_decay = jnp.exp(da_cum[:, :, -1, :])

    def step(h, inp):
        s, d = inp
        return d[:, :, None, None] * h + s, h

    h_final, h_prev = lax.scan(step, h0.astype(x.dtype),
                               (jnp.moveaxis(chunk_states, 1, 0), jnp.moveaxis(chunk_decay, 1, 0)))
    h_prev = jnp.moveaxis(h_prev, 0, 1)
    y_off = jnp.einsum('bcqhn,bchpn,bcqh->bcqhp', cm, h_prev, jnp.exp(da_cum))
    return (y_diag + y_off).reshape(bsz, l, nh, p), h_final


def ssd_group(z, xbc, dt_raw, conv_state, h0, chunk_len, conv_w, conv_b, dt_bias, a_log, d_skip, ssd_norm_g):
    xbc, new_conv = causal_conv(xbc, conv_state, conv_w, conv_b)
    xbc = jax.nn.silu(xbc)
    xs, bm, cm = jnp.split(xbc, [SSD_WIDTH, SSD_WIDTH + N_SSD_GROUPS * D_STATE], axis=-1)
    bsz, l, _ = xs.shape
    xs = xs.reshape(bsz, l, N_SSD_HEADS, SSD_HEAD_DIM)
    bm = jnp.repeat(bm.reshape(bsz, l, N_SSD_GROUPS, D_STATE), HEADS_PER_GROUP, axis=2)
    cm = jnp.repeat(cm.reshape(bsz, l, N_SSD_GROUPS, D_STATE), HEADS_PER_GROUP, axis=2)
    dt = jax.nn.softplus(dt_raw + dt_bias)
    a = -jnp.exp(a_log)
    y, h_final = ssd_scan(xs, dt, a, bm, cm, h0, chunk_len)
    y = (y + d_skip[:, None] * xs).reshape(bsz, l, SSD_WIDTH)
    y = rmsnorm(y * jax.nn.silu(z), ssd_norm_g)
    return y, new_conv, h_final


def rel_bias_lookup(rel_table, rel):
    return rel_table[:, jnp.clip(rel, -MAX_REL, MAX_REL) + MAX_REL].astype(jnp.float32)


def band_attention_prompt(q, k, v, rel_table):
    bsz, l, nh, hd = q.shape
    nc = l // CHUNK
    band = ATTN_PAST + CHUNK
    pad = ((0, 0), (ATTN_PAST, 0), (0, 0), (0, 0))
    kp, vp = jnp.pad(k, pad), jnp.pad(v, pad)
    idx = jnp.arange(nc)[:, None] * CHUNK + jnp.arange(band)[None, :]
    kb, vb = kp[:, idx], vp[:, idx]
    qc = q.reshape(bsz, nc, CHUNK, nh, hd)
    rel = jnp.arange(CHUNK)[:, None] + ATTN_PAST - jnp.arange(band)[None, :]
    bias = rel_bias_lookup(rel_table, rel)
    valid = idx >= ATTN_PAST
    s = jnp.einsum('bcqhd,bckhd->bchqk', qc, kb).astype(jnp.float32) * (ATTN_HEAD_DIM ** -0.5) + bias[None, None]
    s = jnp.where(valid[None, :, None, None, :], s, NEG)
    pr = jax.nn.softmax(s, axis=-1).astype(v.dtype)
    o = jnp.einsum('bchqk,bckhd->bcqhd', pr, vb)
    return o.reshape(bsz, l, nh * hd)


def attention_sample(q, k, v, cache_k, cache_v, rel_table):
    bsz, l, nh, hd = q.shape
    cr = cache_k.shape[1]
    kk = jnp.concatenate([cache_k.astype(k.dtype), k], axis=1)
    vv = jnp.concatenate([cache_v.astype(v.dtype), v], axis=1)
    kpos = jnp.concatenate([jnp.arange(cr) - cr, jnp.arange(l)])
    rel = jnp.arange(l)[:, None] - kpos[None, :]
    bias = rel_bias_lookup(rel_table, rel)
    s = jnp.einsum('bqhd,bkhd->bhqk', q, kk).astype(jnp.float32) * (ATTN_HEAD_DIM ** -0.5) + bias[None]
    pr = jax.nn.softmax(s, axis=-1).astype(v.dtype)
    o = jnp.einsum('bhqk,bkhd->bqhd', pr, vv)
    return o.reshape(bsz, l, nh * hd)


def layer(x, conv_state, ssm_state, cache_k, cache_v, chunk_len,
          norm_mix_g, w_in, conv_w, conv_b, dt_bias, a_log, d_skip, ssd_norm_g,
          q_norm_g, k_norm_g, rel_bias, w_out, norm_ffn_g, w_gate, w_up, w_down):
    bsz, l, _ = x.shape
    h = rmsnorm(x, norm_mix_g)
    u = h @ w_in
    z, xbc, dt_raw, q, k, v = jnp.split(u, IN_SPLITS, axis=-1)
    y_ssd, new_conv, new_ssm = ssd_group(z, xbc, dt_raw, conv_state, ssm_state, chunk_len,
                                         conv_w, conv_b, dt_bias, a_log, d_skip, ssd_norm_g)
    q = rmsnorm(q.reshape(bsz, l, N_ATTN_HEADS, ATTN_HEAD_DIM), q_norm_g)
    k = rmsnorm(k.reshape(bsz, l, N_ATTN_HEADS, ATTN_HEAD_DIM), k_norm_g)
    v = v.reshape(bsz, l, N_ATTN_HEADS, ATTN_HEAD_DIM)
    if cache_k is None:
        o = band_attention_prompt(q, k, v, rel_bias)
        keep = min(ATTN_PAST, l)
        new_k, new_v = k[:, l - keep:], v[:, l - keep:]
    else:
        o = attention_sample(q, k, v, cache_k, cache_v, rel_bias)
        new_k, new_v = k, v
    x = x + jnp.concatenate([y_ssd, o], axis=-1) @ w_out
    f = rmsnorm(x, norm_ffn_g)
    x = x + (jax.nn.silu(f @ w_gate) * (f @ w_up)) @ w_down
    return x, new_k, new_v, new_ssm, new_conv


def setup_inputs(seed: int = 0) -> dict:
    key = jax.random.key(seed)
    ks = jax.random.split(key, 24)
    f32 = jnp.float32
    cache_rows = min(ATTN_PAST, PAST_LEN)
    nrm = lambda k, shape, scale: jax.random.normal(k, shape, f32) * scale
    dt0 = jnp.exp(jax.random.uniform(ks[10], (DEPTH, N_SSD_HEADS), f32, np.log(1e-3), np.log(1e-1)))
    return {
        "x_prompt": nrm(ks[0], (BATCH, SEQ, D_MODEL), 1.0),
        "x_sample": nrm(ks[1], (DEC_BATCH, DEC_SEQ, D_MODEL), 1.0),
        "cache_attn_k": nrm(ks[2], (DEPTH, DEC_BATCH, cache_rows, N_ATTN_HEADS, ATTN_HEAD_DIM), 1.0),
        "cache_attn_v": nrm(ks[3], (DEPTH, DEC_BATCH, cache_rows, N_ATTN_HEADS, ATTN_HEAD_DIM), 1.0),
        "state_ssm": nrm(ks[4], (DEPTH, DEC_BATCH, N_SSD_HEADS, SSD_HEAD_DIM, D_STATE), 0.1),
        "state_conv": nrm(ks[5], (DEPTH, DEC_BATCH, D_CONV - 1, CONV_DIM), 1.0),
        "norm_mix_g": 1.0 + nrm(ks[6], (DEPTH, D_MODEL), 0.02),
        "w_in": nrm(ks[7], (DEPTH, D_MODEL, IN_PROJ), D_MODEL ** -0.5),
        "conv_w": nrm(ks[8], (DEPTH, D_CONV, CONV_DIM), D_CONV ** -0.5),
        "conv_b": nrm(ks[9], (DEPTH, CONV_DIM), 0.02),
        "dt_bias": dt0 + jnp.log(-jnp.expm1(-dt0)),
        "a_log": jnp.log(jax.random.uniform(ks[11], (DEPTH, N_SSD_HEADS), f32, 1.0, 16.0)),
        "d_skip": 1.0 + nrm(ks[12], (DEPTH, N_SSD_HEADS), 0.1),
        "ssd_norm_g": 1.0 + nrm(ks[13], (DEPTH, SSD_WIDTH), 0.02),
        "q_norm_g": 1.0 + nrm(ks[14], (DEPTH, ATTN_HEAD_DIM), 0.02),
        "k_norm_g": 1.0 + nrm(ks[15], (DEPTH, ATTN_HEAD_DIM), 0.02),
        "rel_bias": nrm(ks[16], (DEPTH, N_ATTN_HEADS, 2 * MAX_REL + 1), 0.1),
        "w_out": nrm(ks[17], (DEPTH, D_MIX, D_MODEL), D_MIX ** -0.5),
        "norm_ffn_g": 1.0 + nrm(ks[18], (DEPTH, D_MODEL), 0.02),
        "w_gate": nrm(ks[19], (DEPTH, D_MODEL, D_FF), D_MODEL ** -0.5),
        "w_up": nrm(ks[20], (DEPTH, D_MODEL, D_FF), D_MODEL ** -0.5),
        "w_down": nrm(ks[21], (DEPTH, D_FF, D_MODEL), D_FF ** -0.5),
    }


def reference(x_prompt, x_sample, cache_attn_k, cache_attn_v, state_ssm, state_conv,
              norm_mix_g, w_in, conv_w, conv_b, dt_bias, a_log, d_skip, ssd_norm_g,
              q_norm_g, k_norm_g, rel_bias, w_out, norm_ffn_g, w_gate, w_up, w_down):
    weights = [norm_mix_g, w_in, conv_w, conv_b, dt_bias, a_log, d_skip, ssd_norm_g,
               q_norm_g, k_norm_g, rel_bias, w_out, norm_ffn_g, w_gate, w_up, w_down]
    yp, ys = x_prompt, x_sample
    kp_l, vp_l, sp_l, cp_l, ks_l, vs_l, ss_l, cs_l = [], [], [], [], [], [], [], []
    for i in range(DEPTH):
        wl = [w[i] for w in weights]
        bp = yp.shape[0]
        zero_conv = jnp.zeros((bp, D_CONV - 1, CONV_DIM), yp.dtype)
        zero_ssm = jnp.zeros((bp, N_SSD_HEADS, SSD_HEAD_DIM, D_STATE), yp.dtype)
        yp, kp, vp, sp, cp = layer(yp, zero_conv, zero_ssm, None, None, CHUNK, *wl)
        ys, kn, vn, sn, cn = layer(ys, state_conv[i], state_ssm[i], cache_attn_k[i], cache_attn_v[i],
                                   ys.shape[1], *wl)
        kp_l.append(kp); vp_l.append(vp); sp_l.append(sp); cp_l.append(cp)
        ks_l.append(kn); vs_l.append(vn); ss_l.append(sn); cs_l.append(cn)
    k_prompt, v_prompt = jnp.stack(kp_l), jnp.stack(vp_l)
    ssm_prompt, conv_prompt = jnp.stack(sp_l), jnp.stack(cp_l)
    k_sample, v_sample = jnp.stack(ks_l), jnp.stack(vs_l)
    ssm_sample, conv_sample = jnp.stack(ss_l), jnp.stack(cs_l)
    return (yp, ys, k_prompt, v_prompt, ssm_prompt, conv_prompt, k_sample, v_sample, ssm_sample, conv_sample)
```

```python
import functools

import jax
import jax.numpy as jnp
from jax import lax
from jax.experimental import pallas as pl
from jax.experimental.pallas import tpu as pltpu

F32 = jnp.float32
BF16 = jnp.bfloat16

D_MODEL = 1024
CHUNK = 64
BAND_CHUNKS = 8
ATTN_PAST = BAND_CHUNKS * CHUNK
SSD_WIDTH = 512
HEAD_DIM = 64
N_HEADS = 8
N_GROUPS = 2
HEADS_PER_GROUP = N_HEADS // N_GROUPS
GROUP_WIDTH = HEADS_PER_GROUP * HEAD_DIM
D_STATE = 128
D_CONV = 4
CONV_DIM = SSD_WIDTH + 2 * N_GROUPS * D_STATE
ATTN_WIDTH = 512
MAX_REL = 128
D_FF = 2816
EPS = 1e-6
NEG = -1e30

LANES = 128
SUBLANES = 8
VMEM_LIMIT = 56 * 1024 * 1024
BIAS_COLS = 640
FF_CHUNKS = ((0, 1024), (1024, 1024), (2048, 768))


def _dot(a, b):
    return jnp.dot(a, b, preferred_element_type=F32)


def _dot_nt(a, b):
    return lax.dot_general(a, b, (((1,), (1,)), ((), ())), preferred_element_type=F32)


def _dot_tn(a, b):
    return lax.dot_general(a, b, (((0,), (0,)), ((), ())), preferred_element_type=F32)


def _split3(x):
    hi = x.astype(BF16)
    r1 = x - hi.astype(F32)
    mid = r1.astype(BF16)
    lo = (r1 - mid.astype(F32)).astype(BF16)
    return hi, mid, lo


def _dot_exact_lhs(a, x):
    hi, mid, lo = _split3(x)
    return _dot(a, hi) + _dot(a, mid) + _dot(a, lo)


def _dot_exact_rhs(x, e):
    hi, mid, lo = _split3(x)
    return _dot(hi, e) + _dot(mid, e) + _dot(lo, e)


def _silu(x):
    return x * jax.nn.sigmoid(x)


def _iota(shape, dim):
    return lax.broadcasted_iota(jnp.int32, shape, dim)


def _const_spec(shape):
    nd = len(shape)
    return pl.BlockSpec(shape, lambda *_: (0,) * nd)


def _params(sem):
    return pltpu.CompilerParams(dimension_semantics=sem, vmem_limit_bytes=VMEM_LIMIT)


def _inproj_kernel(x_ref, cs_ref, g_ref, wz_ref, wxbc_ref, wdt_ref, wq_ref, wk_ref, wv_ref,
                   convw_ref, convb_ref, dtb_ref, qg_ref, kg_ref, hm_ref,
                   z_out, xbc_out, dt_out, q_out, k_out, v_out, kf_out, vf_out, conv_out,
                   cbuf, *, tm, keep):
    t = pl.program_id(1)
    last = pl.num_programs(1) - 1
    hist = SUBLANES - (D_CONV - 1)

    x = x_ref[0]
    ms = jnp.mean(x * x, axis=-1, keepdims=True)
    h = ((x * lax.rsqrt(ms + EPS)) * g_ref[...]).astype(BF16)

    z_out[0] = _dot(h, wz_ref[...]).astype(BF16)

    dt_raw = _dot(h, wdt_ref[...]) + dtb_ref[...]
    dt_out[0] = jnp.maximum(dt_raw, 0.0) + jnp.log1p(jnp.exp(-jnp.abs(dt_raw)))

    @pl.when(t == 0)
    def _():
        cbuf[0:SUBLANES, :] = jnp.zeros((SUBLANES, CONV_DIM), F32)
        cbuf[hist:SUBLANES, :] = cs_ref[0]

    xbc = _dot(h, wxbc_ref[...])
    cbuf[SUBLANES:SUBLANES + tm, :] = xbc
    acc = convb_ref[...] + xbc * convw_ref[D_CONV - 1:D_CONV, :]
    for tap in range(D_CONV - 1):
        acc = acc + cbuf[hist + tap:hist + tap + tm, :] * convw_ref[tap:tap + 1, :]
    xbc_out[0] = _silu(acc).astype(BF16)

    @pl.when(t == last)
    def _():
        conv_out[0] = cbuf[tm + hist:tm + SUBLANES, :]

    cbuf[0:SUBLANES, :] = cbuf[tm:tm + SUBLANES, :]

    def head_norm(u, gain):
        ss = _dot((u * u).astype(BF16), hm_ref[...])
        return (u * lax.rsqrt(ss + EPS)) * gain

    qn = head_norm(_dot(h, wq_ref[...]), qg_ref[...])
    q_out[0] = (qn * (HEAD_DIM ** -0.5)).astype(BF16)
    kn = head_norm(_dot(h, wk_ref[...]), kg_ref[...])
    k_out[0] = kn.astype(BF16)
    v = _dot(h, wv_ref[...])
    v_out[0] = v.astype(BF16)

    @pl.when(t == last)
    def _():
        kf_out[0] = kn[tm - keep:, :]
        vf_out[0] = v[tm - keep:, :]


def _in_proj(x, conv_state, p, *, tm):
    bsz, seq, _ = x.shape
    keep = min(ATTN_PAST, seq)
    assert seq % tm == 0 and keep <= tm and seq >= D_CONV - 1
    tok = lambda w: pl.BlockSpec((1, tm, w), lambda b, t: (b, t, 0))
    per_b = lambda r, w: pl.BlockSpec((1, r, w), lambda b, t: (b, 0, 0))
    consts = [p["g_mix"], p["wz"], p["wxbc"], p["wdt"], p["wq"], p["wk"], p["wv"],
              p["conv_w"], p["conv_b"], p["dt_bias"], p["q_gain"], p["k_gain"], p["head_mean"]]
    out_shape = (
        jax.ShapeDtypeStruct((bsz, seq, SSD_WIDTH), BF16),
        jax.ShapeDtypeStruct((bsz, seq, CONV_DIM), BF16),
        jax.ShapeDtypeStruct((bsz, seq, LANES), F32),
        jax.ShapeDtypeStruct((bsz, seq, ATTN_WIDTH), BF16),
        jax.ShapeDtypeStruct((bsz, seq, ATTN_WIDTH), BF16),
        jax.ShapeDtypeStruct((bsz, seq, ATTN_WIDTH), BF16),
        jax.ShapeDtypeStruct((bsz, keep, ATTN_WIDTH), F32),
        jax.ShapeDtypeStruct((bsz, keep, ATTN_WIDTH), F32),
        jax.ShapeDtypeStruct((bsz, D_CONV - 1, CONV_DIM), F32),
    )
    out_specs = (tok(SSD_WIDTH), tok(CONV_DIM), tok(LANES), tok(ATTN_WIDTH), tok(ATTN_WIDTH),
                 tok(ATTN_WIDTH), per_b(keep, ATTN_WIDTH), per_b(keep, ATTN_WIDTH),
                 per_b(D_CONV - 1, CONV_DIM))
    return pl.pallas_call(
        functools.partial(_inproj_kernel, tm=tm, keep=keep),
        grid=(bsz, seq // tm),
        in_specs=[tok(D_MODEL), per_b(D_CONV - 1, CONV_DIM)] + [_const_spec(c.shape) for c in consts],
        out_specs=out_specs,
        out_shape=out_shape,
        scratch_shapes=[pltpu.VMEM((tm + SUBLANES, CONV_DIM), F32)],
        compiler_params=_params(("arbitrary", "arbitrary")),
        name="in_proj",
    )(x, conv_state, *consts)


def _ssd_kernel(xbc_ref, z_ref, dt_ref, h0_ref, alog_p_ref, alog_s_ref, dskip_ref, ng_ref,
                y_out, hfin_out, state, *, q_len, n_chunks):
    t = pl.program_id(1)
    last = pl.num_programs(1) - 1
    hs = N_HEADS * q_len
    gs = HEADS_PER_GROUP * q_len

    @pl.when(t == 0)
    def _():
        state[...] = h0_ref[0].T

    exp_p = (_iota((LANES, SSD_WIDTH), 0) == _iota((LANES, SSD_WIDTH), 1) // HEAD_DIM).astype(BF16)
    exp_s = (_iota((LANES, hs), 0) == _iota((LANES, hs), 1) // q_len).astype(BF16)
    tri = (_iota((q_len, q_len), 0) >= _iota((q_len, q_len), 1)).astype(BF16)
    s_of_lane = _iota((q_len, hs), 1) % q_len
    q_of_row = _iota((q_len, hs), 0)
    causal = q_of_row >= s_of_lane
    after = q_of_row > s_of_lane
    diag = (_iota((gs, GROUP_WIDTH), 0) // q_len) == (_iota((gs, GROUP_WIDTH), 1) // HEAD_DIM)
    a_p = -jnp.exp(alog_p_ref[...])
    a_s = -jnp.exp(alog_s_ref[...])

    def chunk(c, carry):
        rows = pl.ds(pl.multiple_of(c * q_len, q_len), q_len)
        dt = dt_ref[0, rows, :]
        dt_p = _dot_exact_rhs(dt, exp_p)
        da_p = dt_p * a_p
        da_s = da_p if q_len == HEAD_DIM else _dot_exact_rhs(dt, exp_s) * a_s
        cum = _dot_exact_lhs(tri, da_p)
        seg = _dot_exact_lhs(tri, jnp.where(after, da_s, 0.0))
        decay = jnp.where(causal, jnp.exp(jnp.where(causal, seg, 0.0)), 0.0)
        cum_last = cum[q_len - 1:q_len, :]
        to_end = jnp.exp(cum_last - cum)
        from_start = jnp.exp(cum)
        chunk_decay = jnp.exp(cum_last)

        xs = xbc_ref[0, rows, 0:SSD_WIDTH].astype(F32)
        xdt = xs * dt_p
        xdt_b = xdt.astype(BF16)
        x_end = (xdt * to_end).astype(BF16)

        ys = []
        for g in range(N_GROUPS):
            lanes = slice(g * GROUP_WIDTH, (g + 1) * GROUP_WIDTH)
            b_g = xbc_ref[0, rows, SSD_WIDTH + g * D_STATE:SSD_WIDTH + (g + 1) * D_STATE]
            c_g = xbc_ref[0, rows, SSD_WIDTH + (N_GROUPS + g) * D_STATE:
                          SSD_WIDTH + (N_GROUPS + g + 1) * D_STATE]
            cb = _dot_nt(c_g, jnp.concatenate([b_g] * HEADS_PER_GROUP, axis=0))
            scores = (cb * decay[:, g * gs:(g + 1) * gs]).astype(BF16)
            x_diag = jnp.where(diag, jnp.concatenate([xdt_b[:, lanes]] * HEADS_PER_GROUP, axis=0),
                               jnp.zeros((), BF16))
            st = state[:, lanes]
            y_g = _dot(scores, x_diag) + _dot(c_g, st.astype(BF16)) * from_start[:, lanes]
            ys.append(y_g)
            state[:, lanes] = chunk_decay[:, lanes] * st + _dot_tn(b_g, x_end[:, lanes])
        y = jnp.concatenate(ys, axis=1) + dskip_ref[...] * xs
        gated = y * _silu(z_ref[0, rows, :].astype(F32))
        ms = jnp.mean(gated * gated, axis=-1, keepdims=True)
        y_out[0, rows, :] = ((gated * lax.rsqrt(ms + EPS)) * ng_ref[...]).astype(BF16)
        return carry

    lax.fori_loop(0, n_chunks, chunk, 0)

    @pl.when(t == last)
    def _():
        hfin_out[0] = state[...].T


def _ssd(xbc, z, dt, h0, p, *, q_len, tb):
    bsz, seq, _ = xbc.shape
    assert seq % tb == 0 and tb % q_len == 0
    tok = lambda w: pl.BlockSpec((1, tb, w), lambda b, t: (b, t, 0))
    per_b = pl.BlockSpec((1, SSD_WIDTH, D_STATE), lambda b, t: (b, 0, 0))
    alog_s = jnp.repeat(p["a_log"], q_len)[None, :]
    consts = [p["a_log_p"], alog_s, p["d_skip_p"], p["ssd_gain"]]
    return pl.pallas_call(
        functools.partial(_ssd_kernel, q_len=q_len, n_chunks=tb // q_len),
        grid=(bsz, seq // tb),
        in_specs=[tok(CONV_DIM), tok(SSD_WIDTH), tok(LANES), per_b] + [_const_spec(c.shape) for c in consts],
        out_specs=(tok(SSD_WIDTH), per_b),
        out_shape=(jax.ShapeDtypeStruct((bsz, seq, SSD_WIDTH), BF16),
                   jax.ShapeDtypeStruct((bsz, SSD_WIDTH, D_STATE), F32)),
        scratch_shapes=[pltpu.VMEM((D_STATE, SSD_WIDTH), F32)],
        compiler_params=_params(("arbitrary", "arbitrary")),
        name="ssd",
    )(xbc, z, dt, h0, *consts)


def _attn_kernel(q_ref, k_ref, v_ref, fk_ref, fv_ref, rel_ref, o_out, kbuf, vbuf, bias,
                 *, q_len, window, n_chunks, mask_front):
    b = pl.program_id(0)
    t = pl.program_id(1)
    seq = k_ref.shape[1]

    @pl.when((b == 0) & (t == 0))
    def _():
        for h in range(N_HEADS):
            base = jnp.broadcast_to(rel_ref[h:h + 1, :], (q_len, BIAS_COLS))
            bias[h] = pltpu.roll(base, BIAS_COLS - (CHUNK - 1), 1, stride=1, stride_axis=0)

    @pl.when(t == 0)
    def _():
        kbuf[0:ATTN_PAST, :] = fk_ref[0].astype(BF16)
        vbuf[0:ATTN_PAST, :] = fv_ref[0].astype(BF16)
        kbuf[ATTN_PAST:ATTN_PAST + seq, :] = k_ref[0]
        vbuf[ATTN_PAST:ATTN_PAST + seq, :] = v_ref[0]

    lane = _iota((q_len, LANES), 1)
    low = lane < HEAD_DIM
    col = _iota((2 * q_len, window), 1)
    zero = jnp.zeros((), BF16)

    def chunk(c, carry):
        r0 = pl.multiple_of(c * q_len, q_len)
        g0 = pl.multiple_of((t * n_chunks + c) * q_len, q_len)
        first_valid = ATTN_PAST - g0
        for pr in range(N_HEADS // 2):
            lanes = slice(pr * LANES, (pr + 1) * LANES)
            qp = q_ref[0, pl.ds(r0, q_len), lanes]
            lhs = jnp.concatenate([jnp.where(low, qp, zero), jnp.where(low, zero, qp)], axis=0)
            kw = kbuf[pl.ds(g0, window), lanes]
            vw = vbuf[pl.ds(g0, window), lanes]
            s = _dot_nt(lhs, kw)
            s = s + jnp.concatenate([bias[2 * pr, :, 0:window], bias[2 * pr + 1, :, 0:window]], axis=0)
            if mask_front:
                s = jnp.where(col >= first_valid, s, NEG)
            m = jnp.max(s, axis=-1, keepdims=True)
            e = jnp.exp(s - m)
            den = jnp.sum(e, axis=-1, keepdims=True)
            pv = _dot(e.astype(BF16), vw) / den
            o_out[0, pl.ds(r0, q_len), lanes] = jnp.where(low, pv[0:q_len], pv[q_len:]).astype(BF16)
        return carry

    lax.fori_loop(0, n_chunks, chunk, 0)


def _attention(q, k, v, front_k, front_v, rel_rows, *, q_len, tb, mask_front):
    bsz, seq, _ = q.shape
    window = ATTN_PAST + q_len
    assert seq % tb == 0 and tb % q_len == 0 and front_k.shape[1] == ATTN_PAST
    per_front = front_k.shape[0] == bsz
    tok = pl.BlockSpec((1, tb, ATTN_WIDTH), lambda b, t: (b, t, 0))
    full = pl.BlockSpec((1, seq, ATTN_WIDTH), lambda b, t: (b, 0, 0))
    front = pl.BlockSpec((1, ATTN_PAST, ATTN_WIDTH),
                         (lambda b, t: (b, 0, 0)) if per_front else (lambda b, t: (0, 0, 0)))
    return pl.pallas_call(
        functools.partial(_attn_kernel, q_len=q_len, window=window, n_chunks=tb // q_len,
                          mask_front=mask_front),
        grid=(bsz, seq // tb),
        in_specs=[tok, full, full, front, front, _const_spec(rel_rows.shape)],
        out_specs=tok,
        out_shape=jax.ShapeDtypeStruct((bsz, seq, ATTN_WIDTH), BF16),
        scratch_shapes=[pltpu.VMEM((ATTN_PAST + seq, ATTN_WIDTH), BF16),
                        pltpu.VMEM((ATTN_PAST + seq, ATTN_WIDTH), BF16),
                        pltpu.VMEM((N_HEADS, q_len, BIAS_COLS), F32)],
        compiler_params=_params(("arbitrary", "arbitrary")),
        name="attn",
    )(q, k, v, front_k, front_v, rel_rows)


def _outffn_kernel(x_ref, ys_ref, o_ref, wo_ref, g_ref, wg_ref, wu_ref, wd_ref, out_ref, act):
    x1 = (x_ref[...] + _dot(ys_ref[...], wo_ref[0:SSD_WIDTH, :])
          + _dot(o_ref[...], wo_ref[SSD_WIDTH:, :]))
    out_ref[...] = x1
    ms = jnp.mean(x1 * x1, axis=-1, keepdims=True)
    f = ((x1 * lax.rsqrt(ms + EPS)) * g_ref[...]).astype(BF16)
    for start, width in FF_CHUNKS:
        gate = _dot(f, wg_ref[:, start:start + width])
        up = _dot(f, wu_ref[:, start:start + width])
        act[:, start:start + width] = (_silu(gate) * up).astype(BF16)
    out_ref[...] += _dot(act[...], wd_ref[...])


def _out_ffn(x, ys, o, p, *, tm):
    n, _ = x.shape
    assert n % tm == 0
    tok = lambda w: pl.BlockSpec((tm, w), lambda i: (i, 0))
    consts = [p["w_out"], p["g_ffn"], p["w_gate"], p["w_up"], p["w_down"]]
    single = lambda c: pl.BlockSpec(c.shape, lambda i: (0,) * c.ndim, pipeline_mode=pl.Buffered(1))
    return pl.pallas_call(
        _outffn_kernel,
        grid=(n // tm,),
        in_specs=[tok(D_MODEL), tok(SSD_WIDTH), tok(ATTN_WIDTH)] + [single(c) for c in consts],
        out_specs=tok(D_MODEL),
        out_shape=jax.ShapeDtypeStruct((n, D_MODEL), F32),
        scratch_shapes=[pltpu.VMEM((tm, D_FF), BF16)],
        compiler_params=_params(("arbitrary",)),
        name="out_ffn",
    )(x, ys, o, *consts)


def _prep_layer(norm_mix_g, w_in, conv_w, conv_b, dt_bias, a_log, d_skip, ssd_norm_g,
                q_norm_g, k_norm_g, rel_bias, w_out, norm_ffn_g, w_gate, w_up, w_down):
    o_z, o_xbc, o_dt = 0, SSD_WIDTH, SSD_WIDTH + CONV_DIM
    o_q = o_dt + N_HEADS
    o_k, o_v = o_q + ATTN_WIDTH, o_q + 2 * ATTN_WIDTH
    wb = w_in.astype(BF16)
    wdt = jnp.zeros((D_MODEL, LANES), BF16).at[:, :N_HEADS].set(wb[:, o_dt:o_q])
    head_of_lane = jnp.arange(ATTN_WIDTH) // HEAD_DIM
    head_mean = jnp.where(head_of_lane[:, None] == head_of_lane[None, :], 1.0 / HEAD_DIM, 0.0)
    rel_idx = jnp.clip(ATTN_PAST + CHUNK - 1 - jnp.arange(BIAS_COLS), -MAX_REL, MAX_REL) + MAX_REL
    return dict(
        g_mix=norm_mix_g[None, :],
        wz=wb[:, o_z:o_xbc], wxbc=wb[:, o_xbc:o_dt], wdt=wdt,
        wq=wb[:, o_q:o_k], wk=wb[:, o_k:o_v], wv=wb[:, o_v:],
        conv_w=conv_w, conv_b=conv_b[None, :],
        dt_bias=jnp.zeros((1, LANES), F32).at[0, :N_HEADS].set(dt_bias),
        q_gain=jnp.tile(q_norm_g, N_HEADS)[None, :], k_gain=jnp.tile(k_norm_g, N_HEADS)[None, :],
        head_mean=head_mean.astype(BF16),
        a_log=a_log, a_log_p=jnp.repeat(a_log, HEAD_DIM)[None, :],
        d_skip_p=jnp.repeat(d_skip, HEAD_DIM)[None, :], ssd_gain=ssd_norm_g[None, :],
        rel_rows=rel_bias[:, rel_idx],
        w_out=w_out.astype(BF16), g_ffn=norm_ffn_g[None, :],
        w_gate=w_gate.astype(BF16), w_up=w_up.astype(BF16), w_down=w_down.astype(BF16),
    )


def _layer(x, conv_state, ssm_state, cache_k, cache_v, chunk_len, p):
    bsz, seq, _ = x.shape
    tile = min(512, seq)
    z, xbc, dt, q, k, v, k_new, v_new, conv_new = _in_proj(x, conv_state, p, tm=tile)
    h0 = ssm_state.reshape(bsz, SSD_WIDTH, D_STATE)
    y_ssd, h_fin = _ssd(xbc, z, dt, h0, p, q_len=chunk_len, tb=tile)
    if cache_k is None:
        front_k = front_v = jnp.zeros((1, ATTN_PAST, ATTN_WIDTH), F32)
        q_len = CHUNK
    else:
        front_k = cache_k.reshape(bsz, ATTN_PAST, ATTN_WIDTH)
        front_v = cache_v.reshape(bsz, ATTN_PAST, ATTN_WIDTH)
        q_len = seq
    o = _attention(q, k, v, front_k, front_v, p["rel_rows"], q_len=q_len, tb=tile,
                   mask_front=cache_k is None)
    n = bsz * seq
    out = _out_ffn(x.reshape(n, D_MODEL), y_ssd.reshape(n, SSD_WIDTH), o.reshape(n, ATTN_WIDTH),
                   p, tm=min(512, n))
    keep = k_new.shape[1]
    return (out.reshape(bsz, seq, D_MODEL),
            k_new.reshape(bsz, keep, N_HEADS, HEAD_DIM), v_new.reshape(bsz, keep, N_HEADS, HEAD_DIM),
            h_fin.reshape(bsz, N_HEADS, HEAD_DIM, D_STATE), conv_new)


def kernel(x_prompt, x_sample, cache_attn_k, cache_attn_v, state_ssm, state_conv, norm_mix_g, w_in, conv_w, conv_b, dt_bias, a_log, d_skip, ssd_norm_g, q_norm_g, k_norm_g, rel_bias, w_out, norm_ffn_g, w_gate, w_up, w_down):
    weights = [norm_mix_g, w_in, conv_w, conv_b, dt_bias, a_log, d_skip, ssd_norm_g,
               q_norm_g, k_norm_g, rel_bias, w_out, norm_ffn_g, w_gate, w_up, w_down]
    depth = w_in.shape[0]
    assert cache_attn_k.shape[2] == ATTN_PAST
    yp, ys = x_prompt, x_sample
    outs = [[] for _ in range(8)]
    for i in range(depth):
        p = _prep_layer(*[w[i] for w in weights])
        bp = yp.shape[0]
        zero_conv = jnp.zeros((bp, D_CONV - 1, CONV_DIM), F32)
        zero_ssm = jnp.zeros((bp, N_HEADS, HEAD_DIM, D_STATE), F32)
        yp, kp, vp, sp, cp = _layer(yp, zero_conv, zero_ssm, None, None, CHUNK, p)
        ys, kn, vn, sn, cn = _layer(ys, state_conv[i], state_ssm[i], cache_attn_k[i], cache_attn_v[i],
                                    ys.shape[1], p)
        for lst, val in zip(outs, (kp, vp, sp, cp, kn, vn, sn, cn)):
            lst.append(val)
    return (yp, ys) + tuple(jnp.stack(lst) for lst in outs)
```

```python
import functools

import jax
import jax.numpy as jnp
from jax import lax
from jax.experimental import pallas as pl
from jax.experimental.pallas import tpu as pltpu

F32 = jnp.float32
BF16 = jnp.bfloat16

D_MODEL = 1024
CHUNK = 64
BAND_CHUNKS = 8
ATTN_PAST = BAND_CHUNKS * CHUNK
SSD_WIDTH = 512
HEAD_DIM = 64
N_HEADS = 8
N_GROUPS = 2
HEADS_PER_GROUP = N_HEADS // N_GROUPS
GROUP_WIDTH = HEADS_PER_GROUP * HEAD_DIM
D_STATE = 128
D_CONV = 4
CONV_DIM = SSD_WIDTH + 2 * N_GROUPS * D_STATE
ATTN_WIDTH = 512
MAX_REL = 128
D_FF = 2816
EPS = 1e-6
NEG = -1e30
LOG2E = 1.4426950408889634

LANES = 128
SUBLANES = 8
VMEM_LIMIT = 56 * 1024 * 1024
FFN_TILE = 512
INPROJ_TILE = 1024
INPROJ_SUB = 256
SSD_CHUNK = 128
BIAS_COLS = 640
FF_CHUNKS = ((0, 1024), (1024, 1024), (2048, 768))


def _dot(a, b):
    return jnp.dot(a, b, preferred_element_type=F32)


def _dot_nt(a, b):
    return lax.dot_general(a, b, (((1,), (1,)), ((), ())), preferred_element_type=F32)


def _dot_tn(a, b):
    return lax.dot_general(a, b, (((0,), (0,)), ((), ())), preferred_element_type=F32)


def _dot_exact_lhs(a, x):
    hi = x.astype(BF16)
    r1 = x - hi.astype(F32)
    mid = r1.astype(BF16)
    lo = (r1 - mid.astype(F32)).astype(BF16)
    return _dot(a, hi) + _dot(a, mid) + _dot(a, lo)


def _silu(x):
    return x * jax.nn.sigmoid(x)


def _iota(shape, dim):
    return lax.broadcasted_iota(jnp.int32, shape, dim)


def _rows(i, n):
    return pl.ds(i * n if isinstance(i, int) else pl.multiple_of(i * n, n), n)


def _const_spec(shape):
    nd = len(shape)
    return pl.BlockSpec(shape, lambda *_: (0,) * nd)


def _params(sem):
    return pltpu.CompilerParams(dimension_semantics=sem, vmem_limit_bytes=VMEM_LIMIT)


def _inproj_kernel(x_ref, cs_ref, g_ref, wz_ref, wxbc_ref, wdt_ref, wq_ref, wk_ref, wv_ref,
                   convw_ref, convb_ref, dtb_ref, qg_ref, kg_ref,
                   z_out, xbc_out, dt_out, q_out, k_out, v_out, kf_out, vf_out, conv_out,
                   cbuf, hbuf, qbuf, kbuf, vbuf, *, tm, sub, keep):
    t = pl.program_id(1)
    last = pl.num_programs(1) - 1
    n_sub = tm // sub
    n_slab = CONV_DIM // LANES
    hist = SUBLANES - (D_CONV - 1)
    low = _iota((sub, LANES), 1) < HEAD_DIM

    @pl.when(t == 0)
    def _():
        for j in range(n_slab):
            cbuf[j, 0:SUBLANES, :] = jnp.zeros((SUBLANES, LANES), F32)
            cbuf[j, hist:SUBLANES, :] = cs_ref[0, :, j * LANES:(j + 1) * LANES]

    def norm(i):
        x = x_ref[0, _rows(i, sub), :]
        ms = jnp.mean(x * x, axis=-1, keepdims=True)
        hbuf[i % 2] = ((x * lax.rsqrt(ms + EPS)) * g_ref[...]).astype(BF16)

    def project(i):
        rows = _rows(i, sub)
        h = hbuf[i % 2]
        xbc = _dot(h, wxbc_ref[...])
        for j in range(n_slab):
            cbuf[j, pl.ds(SUBLANES + i * sub, sub), :] = xbc[:, j * LANES:(j + 1) * LANES]
        qbuf[i % 2] = _dot(h, wq_ref[...])
        kbuf[i % 2] = _dot(h, wk_ref[...])
        z_out[0, rows, :] = _dot(h, wz_ref[...]).astype(BF16)
        v = _dot(h, wv_ref[...])
        v_out[0, rows, :] = v.astype(BF16)
        vbuf[i % 2] = v
        dt_raw = _dot(h, wdt_ref[...]) + dtb_ref[...]
        dt_out[0, rows, :] = jnp.maximum(dt_raw, 0.0) + jnp.log1p(jnp.exp(-jnp.abs(dt_raw)))

    def head_norm(u, gain):
        out = []
        for pr in range(N_HEADS // 2):
            up = u[:, pr * LANES:(pr + 1) * LANES]
            sq = up * up
            ms_lo = jnp.sum(jnp.where(low, sq, 0.0), axis=-1, keepdims=True) * (1.0 / HEAD_DIM)
            ms_hi = jnp.sum(jnp.where(low, 0.0, sq), axis=-1, keepdims=True) * (1.0 / HEAD_DIM)
            out.append(up * lax.rsqrt(jnp.where(low, ms_lo, ms_hi) + EPS))
        return jnp.concatenate(out, axis=1) * gain

    def finish(i):
        rows = _rows(i, sub)
        for j in range(n_slab):
            sl = slice(j * LANES, (j + 1) * LANES)
            acc = convb_ref[:, sl]
            for tap in range(D_CONV):
                acc = acc + cbuf[j, pl.ds(hist + tap + i * sub, sub), :] * convw_ref[tap:tap + 1, sl]
            xbc_out[0, rows, sl] = _silu(acc).astype(BF16)
        qn = head_norm(qbuf[i % 2], qg_ref[...])
        q_out[0, rows, :] = (qn * (HEAD_DIM ** -0.5 * LOG2E)).astype(BF16)
        kn = head_norm(kbuf[i % 2], kg_ref[...])
        k_out[0, rows, :] = kn.astype(BF16)
        newest = pl.ds(max(i * sub - (tm - keep), 0), sub)
        kf_out[0, newest, :] = kn
        vf_out[0, newest, :] = vbuf[i % 2]

    norm(0)
    for i in range(n_sub):
        if i > 0:
            finish(i - 1)
        project(i)
        if i + 1 < n_sub:
            norm(i + 1)
    finish(n_sub - 1)

    @pl.when(t == last)
    def _():
        for j in range(n_slab):
            conv_out[0, :, j * LANES:(j + 1) * LANES] = cbuf[j, tm + hist:tm + SUBLANES, :]

    for j in range(n_slab):
        cbuf[j, 0:SUBLANES, :] = cbuf[j, tm:tm + SUBLANES, :]


def _in_proj(x, conv_state, p, *, tm, sub):
    bsz, seq, _ = x.shape
    keep = min(ATTN_PAST, seq)
    assert seq % tm == 0 and tm % sub == 0 and keep % sub == 0 and keep <= tm and seq >= D_CONV - 1
    tok = lambda w: pl.BlockSpec((1, tm, w), lambda b, t: (b, t, 0))
    per_b = lambda r, w: pl.BlockSpec((1, r, w), lambda b, t: (b, 0, 0))
    consts = [p["g_mix"], p["wz"], p["wxbc"], p["wdt"], p["wq"], p["wk"], p["wv"],
              p["conv_w"], p["conv_b"], p["dt_bias"], p["q_gain"], p["k_gain"]]
    out_shape = (
        jax.ShapeDtypeStruct((bsz, seq, SSD_WIDTH), BF16),
        jax.ShapeDtypeStruct((bsz, seq, CONV_DIM), BF16),
        jax.ShapeDtypeStruct((bsz, seq, LANES), F32),
        jax.ShapeDtypeStruct((bsz, seq, ATTN_WIDTH), BF16),
        jax.ShapeDtypeStruct((bsz, seq, ATTN_WIDTH), BF16),
        jax.ShapeDtypeStruct((bsz, seq, ATTN_WIDTH), BF16),
        jax.ShapeDtypeStruct((bsz, keep, ATTN_WIDTH), F32),
        jax.ShapeDtypeStruct((bsz, keep, ATTN_WIDTH), F32),
        jax.ShapeDtypeStruct((bsz, D_CONV - 1, CONV_DIM), F32),
    )
    out_specs = (tok(SSD_WIDTH), tok(CONV_DIM), tok(LANES), tok(ATTN_WIDTH), tok(ATTN_WIDTH),
                 tok(ATTN_WIDTH), per_b(keep, ATTN_WIDTH), per_b(keep, ATTN_WIDTH),
                 per_b(D_CONV - 1, CONV_DIM))
    return pl.pallas_call(
        functools.partial(_inproj_kernel, tm=tm, sub=sub, keep=keep),
        grid=(bsz, seq // tm),
        in_specs=[tok(D_MODEL), per_b(D_CONV - 1, CONV_DIM)] + [_const_spec(c.shape) for c in consts],
        out_specs=out_specs,
        out_shape=out_shape,
        scratch_shapes=[pltpu.VMEM((CONV_DIM // LANES, tm + SUBLANES, LANES), F32),
                        pltpu.VMEM((2, sub, D_MODEL), BF16),
                        pltpu.VMEM((2, sub, ATTN_WIDTH), F32),
                        pltpu.VMEM((2, sub, ATTN_WIDTH), F32),
                        pltpu.VMEM((2, sub, ATTN_WIDTH), F32)],
        compiler_params=_params(("arbitrary", "arbitrary")),
        name="in_proj",
    )(x, conv_state, *consts)


def _ssd_kernel(xbc_ref, z_ref, dt_ref, h0_ref, alog_h_ref, dskip_ref, ng_ref,
                y_out, hfin_out, state, sc_buf, xdt_buf, xend_buf, grow_buf, cdec_buf, *, q_len, n_chunks):
    n_pairs = N_HEADS // 2
    state[...] = h0_ref[0].T

    causal = _iota((q_len, q_len), 0) >= _iota((q_len, q_len), 1)
    tri = causal.astype(BF16)
    low = _iota((q_len, LANES), 1) < HEAD_DIM
    a_h = -jnp.exp(alog_h_ref[...]) * LOG2E
    zero = jnp.zeros((), BF16)

    def b_of(rows, g):
        return xbc_ref[0, rows, SSD_WIDTH + g * D_STATE:SSD_WIDTH + (g + 1) * D_STATE]

    def c_of(rows, g):
        return xbc_ref[0, rows, SSD_WIDTH + (N_GROUPS + g) * D_STATE:SSD_WIDTH + (N_GROUPS + g + 1) * D_STATE]

    def on_hp_lanes(cols):
        return jnp.concatenate(
            [jnp.where(low, cols[2 * i][:, 0:LANES], cols[2 * i + 1][:, 0:LANES]) for i in range(n_pairs)], axis=1)

    def local(c, slot):
        rows = _rows(c, q_len)
        dt = dt_ref[0, rows, :]
        cum_h = _dot_exact_lhs(tri, dt * a_h)
        cum_t = cum_h.T
        cum_b = [jnp.broadcast_to(cum_h[:, h:h + 1], (q_len, q_len)) for h in range(N_HEADS)]
        dt_b = [jnp.broadcast_to(dt[:, h:h + 1], (q_len, LANES)) for h in range(N_HEADS)]
        cum = on_hp_lanes(cum_b)
        cum_last = cum[q_len - 1:q_len, :]
        xdt = xbc_ref[0, rows, 0:SSD_WIDTH].astype(F32) * on_hp_lanes(dt_b)
        xdt_buf[slot] = xdt.astype(BF16)
        xend_buf[slot] = (xdt * jnp.exp2(cum_last - cum)).astype(BF16)
        grow_buf[slot] = jnp.exp2(cum)
        cdec_buf[slot] = jnp.exp2(cum_last)
        for g in range(N_GROUPS):
            cb = _dot_nt(c_of(rows, g), b_of(rows, g))
            for h in range(g * HEADS_PER_GROUP, (g + 1) * HEADS_PER_GROUP):
                seg = cum_b[h] - jnp.broadcast_to(cum_t[h:h + 1, :], (q_len, q_len))
                sc_buf[slot, h // 2, :, (h % 2) * q_len:(h % 2 + 1) * q_len] = (
                    cb * jnp.exp2(jnp.where(causal, seg, NEG))).astype(BF16)

    def carried(c, slot):
        rows = _rows(c, q_len)
        ys = []
        for g in range(N_GROUPS):
            lanes = slice(g * GROUP_WIDTH, (g + 1) * GROUP_WIDTH)
            st = state[:, lanes]
            y_diag = []
            for pr in range(g * HEADS_PER_GROUP // 2, (g + 1) * HEADS_PER_GROUP // 2):
                xp = xdt_buf[slot, :, pr * LANES:(pr + 1) * LANES]
                rhs = jnp.concatenate([jnp.where(low, xp, zero), jnp.where(low, zero, xp)], axis=0)
                y_diag.append(_dot(sc_buf[slot, pr], rhs))
            ys.append(jnp.concatenate(y_diag, axis=1)
                      + _dot(c_of(rows, g), st.astype(BF16)) * grow_buf[slot, :, lanes])
            state[:, lanes] = cdec_buf[slot, :, lanes] * st + _dot_tn(b_of(rows, g), xend_buf[slot, :, lanes])
        xs = xbc_ref[0, rows, 0:SSD_WIDTH].astype(F32)
        y = jnp.concatenate(ys, axis=1) + dskip_ref[...] * xs
        gated = y * _silu(z_ref[0, rows, :].astype(F32))
        ms = jnp.mean(gated * gated, axis=-1, keepdims=True)
        y_out[0, rows, :] = ((gated * lax.rsqrt(ms + EPS)) * ng_ref[...]).astype(BF16)

    local(0, 0)
    if n_chunks > 1:
        def two_chunks(c2, carry):
            local(2 * c2 + 1, 1)
            carried(2 * c2, 0)
            local(2 * c2 + 2, 0)
            carried(2 * c2 + 1, 1)
            return carry
        lax.fori_loop(0, n_chunks // 2 - 1, two_chunks, 0)
        local(n_chunks - 1, 1)
        carried(n_chunks - 2, 0)
    carried(n_chunks - 1, (n_chunks - 1) % 2)
    hfin_out[0] = state[...].T


def _ssd(xbc, z, dt, h0, p, *, q_len):
    bsz, seq, _ = xbc.shape
    n_chunks = seq // q_len
    assert seq % q_len == 0 and q_len % LANES == 0 and (n_chunks == 1 or n_chunks % 2 == 0)
    full = lambda w: pl.BlockSpec((1, seq, w), lambda b: (b, 0, 0))
    per_b = pl.BlockSpec((1, SSD_WIDTH, D_STATE), lambda b: (b, 0, 0))
    consts = [p["a_log_h"], p["d_skip_p"], p["ssd_gain"]]
    return pl.pallas_call(
        functools.partial(_ssd_kernel, q_len=q_len, n_chunks=n_chunks),
        grid=(bsz,),
        in_specs=[full(CONV_DIM), full(SSD_WIDTH), full(LANES), per_b] + [_const_spec(c.shape) for c in consts],
        out_specs=(full(SSD_WIDTH), per_b),
        out_shape=(jax.ShapeDtypeStruct((bsz, seq, SSD_WIDTH), BF16),
                   jax.ShapeDtypeStruct((bsz, SSD_WIDTH, D_STATE), F32)),
        scratch_shapes=[pltpu.VMEM((D_STATE, SSD_WIDTH), F32),
                        pltpu.VMEM((2, N_HEADS // 2, q_len, 2 * q_len), BF16),
                        pltpu.VMEM((2, q_len, SSD_WIDTH), BF16),
                        pltpu.VMEM((2, q_len, SSD_WIDTH), BF16),
                        pltpu.VMEM((2, q_len, SSD_WIDTH), F32),
                        pltpu.VMEM((2, 1, SSD_WIDTH), F32)],
        compiler_params=_params(("arbitrary",)),
        name="ssd",
    )(xbc, z, dt, h0, *consts)


def _attn_kernel(q_ref, k_ref, v_ref, fk_ref, fv_ref, rel_ref, o_out, kbuf, vbuf, bias, sbuf, mbuf,
                 *, q_len, window, n_chunks, n_masked):
    seq = k_ref.shape[1]
    n_pairs = N_HEADS // 2

    @pl.when(pl.program_id(0) == 0)
    def _():
        for h in range(N_HEADS):
            base = jnp.broadcast_to(rel_ref[h:h + 1, :], (q_len, BIAS_COLS))
            rows = pltpu.roll(base, BIAS_COLS - (CHUNK - 1), 1, stride=1, stride_axis=0)
            bias[h // 2, (h % 2) * q_len:(h % 2 + 1) * q_len, :] = rows * LOG2E

    kbuf[0:ATTN_PAST, :] = fk_ref[0].astype(BF16)
    vbuf[0:ATTN_PAST, :] = fv_ref[0].astype(BF16)
    kbuf[ATTN_PAST:ATTN_PAST + seq, :] = k_ref[0]
    vbuf[ATTN_PAST:ATTN_PAST + seq, :] = v_ref[0]

    low = _iota((q_len, LANES), 1) < HEAD_DIM
    col = _iota((2 * q_len, window), 1)
    zero = jnp.zeros((), BF16)
    ones = jnp.ones((window, LANES), BF16)

    def row0(c):
        return c * q_len if isinstance(c, int) else pl.multiple_of(c * q_len, q_len)

    def scores(c, slot, masked):
        r0 = row0(c)
        for pr in range(n_pairs):
            lanes = slice(pr * LANES, (pr + 1) * LANES)
            qp = q_ref[0, pl.ds(r0, q_len), lanes]
            lhs = jnp.concatenate([jnp.where(low, qp, zero), jnp.where(low, zero, qp)], axis=0)
            s = _dot_nt(lhs, kbuf[pl.ds(r0, window), lanes]) + bias[pr, :, 0:window]
            if masked:
                s = jnp.where(col >= ATTN_PAST - r0, s, NEG)
            sbuf[slot, pr] = s
            mbuf[slot, pr] = jnp.max(s, axis=-1, keepdims=True)

    def finish(c, slot):
        r0 = row0(c)
        for pr in range(n_pairs):
            lanes = slice(pr * LANES, (pr + 1) * LANES)
            e = jnp.exp2(sbuf[slot, pr] - mbuf[slot, pr]).astype(BF16)
            pv = _dot(e, jnp.concatenate([vbuf[pl.ds(r0, window), lanes], ones], axis=1))
            pv = pv[:, 0:LANES] / pv[:, LANES:]
            o_out[0, pl.ds(r0, q_len), lanes] = jnp.where(low, pv[0:q_len], pv[q_len:]).astype(BF16)

    scores(0, 0, n_masked > 0)
    if n_chunks > 1:
        def two_chunks(masked):
            def body(c2, carry):
                scores(2 * c2 + 1, 1, masked)
                finish(2 * c2, 0)
                scores(2 * c2 + 2, 0, masked)
                finish(2 * c2 + 1, 1)
                return carry
            return body
        n_steps = n_chunks // 2 - 1
        n_masked_steps = min((n_masked + 1) // 2, n_steps)
        lax.fori_loop(0, n_masked_steps, two_chunks(True), 0)
        lax.fori_loop(n_masked_steps, n_steps, two_chunks(False), 0)
        scores(n_chunks - 1, 1, n_masked > n_chunks - 1)
        finish(n_chunks - 2, 0)
    finish(n_chunks - 1, (n_chunks - 1) % 2)


def _attention(q, k, v, front_k, front_v, rel_rows, *, q_len, mask_front):
    bsz, seq, _ = q.shape
    window = ATTN_PAST + q_len
    n_chunks = seq // q_len
    assert seq % q_len == 0 and front_k.shape[1] == ATTN_PAST and (n_chunks == 1 or n_chunks % 2 == 0)
    per_front = front_k.shape[0] == bsz
    full = pl.BlockSpec((1, seq, ATTN_WIDTH), lambda b: (b, 0, 0))
    front = pl.BlockSpec((1, ATTN_PAST, ATTN_WIDTH),
                         (lambda b: (b, 0, 0)) if per_front else (lambda b: (0, 0, 0)))
    return pl.pallas_call(
        functools.partial(_attn_kernel, q_len=q_len, window=window, n_chunks=n_chunks,
                          n_masked=min(BAND_CHUNKS, n_chunks) if mask_front else 0),
        grid=(bsz,),
        in_specs=[full, full, full, front, front, _const_spec(rel_rows.shape)],
        out_specs=full,
        out_shape=jax.ShapeDtypeStruct((bsz, seq, ATTN_WIDTH), BF16),
        scratch_shapes=[pltpu.VMEM((ATTN_PAST + seq, ATTN_WIDTH), BF16),
                        pltpu.VMEM((ATTN_PAST + seq, ATTN_WIDTH), BF16),
                        pltpu.VMEM((N_HEADS // 2, 2 * q_len, BIAS_COLS), F32),
                        pltpu.VMEM((2, N_HEADS // 2, 2 * q_len, window), F32),
                        pltpu.VMEM((2, N_HEADS // 2, 2 * q_len, 1), F32)],
        compiler_params=_params(("arbitrary",)),
        name="attn",
    )(q, k, v, front_k, front_v, rel_rows)


def _outffn_kernel(x_ref, ys_ref, o_ref, wo_ref, g_ref, wg_ref, wu_ref, wd_ref, out_ref, act):
    x1 = (x_ref[...] + _dot(ys_ref[...], wo_ref[0:SSD_WIDTH, :])
          + _dot(o_ref[...], wo_ref[SSD_WIDTH:, :]))
    out_ref[...] = x1
    ms = jnp.mean(x1 * x1, axis=-1, keepdims=True)
    f = ((x1 * lax.rsqrt(ms + EPS)) * g_ref[...]).astype(BF16)
    for start, width in FF_CHUNKS:
        gate = _dot(f, wg_ref[:, start:start + width])
        up = _dot(f, wu_ref[:, start:start + width])
        act[:, start:start + width] = (_silu(gate) * up).astype(BF16)
    out_ref[...] += _dot(act[...], wd_ref[...])


def _out_ffn(x, ys, o, p, *, tm):
    n, _ = x.shape
    assert n % tm == 0
    tok = lambda w: pl.BlockSpec((tm, w), lambda i: (i, 0))
    consts = [p["w_out"], p["g_ffn"], p["w_gate"], p["w_up"], p["w_down"]]
    single = lambda c: pl.BlockSpec(c.shape, lambda i: (0,) * c.ndim, pipeline_mode=pl.Buffered(1))
    return pl.pallas_call(
        _outffn_kernel,
        grid=(n // tm,),
        in_specs=[tok(D_MODEL), tok(SSD_WIDTH), tok(ATTN_WIDTH)] + [single(c) for c in consts],
        out_specs=tok(D_MODEL),
        out_shape=jax.ShapeDtypeStruct((n, D_MODEL), F32),
        scratch_shapes=[pltpu.VMEM((tm, D_FF), BF16)],
        compiler_params=_params(("arbitrary",)),
        name="out_ffn",
    )(x, ys, o, *consts)


def _prep_layer(norm_mix_g, w_in, conv_w, conv_b, dt_bias, a_log, d_skip, ssd_norm_g,
                q_norm_g, k_norm_g, rel_bias, w_out, norm_ffn_g, w_gate, w_up, w_down):
    o_z, o_xbc, o_dt = 0, SSD_WIDTH, SSD_WIDTH + CONV_DIM
    o_q = o_dt + N_HEADS
    o_k, o_v = o_q + ATTN_WIDTH, o_q + 2 * ATTN_WIDTH
    wb = w_in.astype(BF16)
    wdt = jnp.zeros((D_MODEL, LANES), BF16).at[:, :N_HEADS].set(wb[:, o_dt:o_q])
    rel_idx = jnp.clip(ATTN_PAST + CHUNK - 1 - jnp.arange(BIAS_COLS), -MAX_REL, MAX_REL) + MAX_REL
    return dict(
        g_mix=norm_mix_g[None, :],
        wz=wb[:, o_z:o_xbc], wxbc=wb[:, o_xbc:o_dt], wdt=wdt,
        wq=wb[:, o_q:o_k], wk=wb[:, o_k:o_v], wv=wb[:, o_v:],
        conv_w=conv_w, conv_b=conv_b[None, :],
        dt_bias=jnp.zeros((1, LANES), F32).at[0, :N_HEADS].set(dt_bias),
        q_gain=jnp.tile(q_norm_g, N_HEADS)[None, :], k_gain=jnp.tile(k_norm_g, N_HEADS)[None, :],
        a_log_h=jnp.zeros((1, LANES), F32).at[0, :N_HEADS].set(a_log),
        d_skip_p=jnp.repeat(d_skip, HEAD_DIM)[None, :], ssd_gain=ssd_norm_g[None, :],
        rel_rows=rel_bias[:, rel_idx],
        w_out=w_out.astype(BF16), g_ffn=norm_ffn_g[None, :],
        w_gate=w_gate.astype(BF16), w_up=w_up.astype(BF16), w_down=w_down.astype(BF16),
    )


def _layer(x, conv_state, ssm_state, cache_k, cache_v, p):
    bsz, seq, _ = x.shape
    z, xbc, dt, q, k, v, k_new, v_new, conv_new = _in_proj(x, conv_state, p, tm=min(INPROJ_TILE, seq),
                                                           sub=min(INPROJ_SUB, seq))
    h0 = ssm_state.reshape(bsz, SSD_WIDTH, D_STATE)
    pad = (-seq) % SSD_CHUNK
    pad_rows = lambda a: jnp.pad(a, ((0, 0), (0, pad), (0, 0))) if pad else a
    y_ssd, h_fin = _ssd(pad_rows(xbc), pad_rows(z), pad_rows(dt), h0, p, q_len=SSD_CHUNK)
    y_ssd = y_ssd[:, :seq]
    if cache_k is None:
        front_k = front_v = jnp.zeros((1, ATTN_PAST, ATTN_WIDTH), F32)
        q_len = CHUNK
    else:
        front_k = cache_k.reshape(bsz, ATTN_PAST, ATTN_WIDTH)
        front_v = cache_v.reshape(bsz, ATTN_PAST, ATTN_WIDTH)
        q_len = seq
    o = _attention(q, k, v, front_k, front_v, p["rel_rows"], q_len=q_len, mask_front=cache_k is None)
    n = bsz * seq
    out = _out_ffn(x.reshape(n, D_MODEL), y_ssd.reshape(n, SSD_WIDTH), o.reshape(n, ATTN_WIDTH),
                   p, tm=min(FFN_TILE, n))
    keep = k_new.shape[1]
    return (out.reshape(bsz, seq, D_MODEL),
            k_new.reshape(bsz, keep, N_HEADS, HEAD_DIM), v_new.reshape(bsz, keep, N_HEADS, HEAD_DIM),
            h_fin.reshape(bsz, N_HEADS, HEAD_DIM, D_STATE), conv_new)


def kernel(x_prompt, x_sample, cache_attn_k, cache_attn_v, state_ssm, state_conv, norm_mix_g, w_in, conv_w, conv_b, dt_bias, a_log, d_skip, ssd_norm_g, q_norm_g, k_norm_g, rel_bias, w_out, norm_ffn_g, w_gate, w_up, w_down):
    weights = [norm_mix_g, w_in, conv_w, conv_b, dt_bias, a_log, d_skip, ssd_norm_g,
               q_norm_g, k_norm_g, rel_bias, w_out, norm_ffn_g, w_gate, w_up, w_down]
    depth = w_in.shape[0]
    assert cache_attn_k.shape[2] == ATTN_PAST
    yp, ys = x_prompt, x_sample
    outs = [[] for _ in range(8)]
    for i in range(depth):
        p = _prep_layer(*[w[i] for w in weights])
        bp = yp.shape[0]
        zero_conv = jnp.zeros((bp, D_CONV - 1, CONV_DIM), F32)
        zero_ssm = jnp.zeros((bp, N_HEADS, HEAD_DIM, D_STATE), F32)
        yp, kp, vp, sp, cp = _layer(yp, zero_conv, zero_ssm, None, None, p)
        ys, kn, vn, sn, cn = _layer(ys, state_conv[i], state_ssm[i], cache_attn_k[i], cache_attn_v[i], p)
        for lst, val in zip(outs, (kp, vp, sp, cp, kn, vn, sn, cn)):
            lst.append(val)
    return (yp, ys) + tuple(jnp.stack(lst) for lst in outs)
```

```python
import functools

import jax
import jax.numpy as jnp
from jax import lax
from jax.experimental import pallas as pl
from jax.experimental.pallas import tpu as pltpu

F32 = jnp.float32
BF16 = jnp.bfloat16

D_MODEL = 1024
CHUNK = 64
BAND_CHUNKS = 8
ATTN_PAST = BAND_CHUNKS * CHUNK
SSD_WIDTH = 512
HEAD_DIM = 64
N_HEADS = 8
N_GROUPS = 2
HEADS_PER_GROUP = N_HEADS // N_GROUPS
GROUP_WIDTH = HEADS_PER_GROUP * HEAD_DIM
D_STATE = 128
D_CONV = 4
CONV_DIM = SSD_WIDTH + 2 * N_GROUPS * D_STATE
ATTN_WIDTH = 512
MAX_REL = 128
D_FF = 2816
EPS = 1e-6
NEG = -1e30
LOG2E = 1.4426950408889634

LANES = 128
SUBLANES = 8
VMEM_LIMIT = 56 * 1024 * 1024
FFN_TILE = 512
INPROJ_TILE = 1024
INPROJ_SUB = 256
SSD_CHUNK = 128
BIAS_COLS = 640
FF_CHUNKS = ((0, 1024), (1024, 1024), (2048, 768))


def _dot(a, b):
    return jnp.dot(a, b, preferred_element_type=F32)


def _dot_nt(a, b):
    return lax.dot_general(a, b, (((1,), (1,)), ((), ())), preferred_element_type=F32)


def _dot_tn(a, b):
    return lax.dot_general(a, b, (((0,), (0,)), ((), ())), preferred_element_type=F32)


def _dot_exact_lhs(a, x):
    hi = x.astype(BF16)
    r1 = x - hi.astype(F32)
    mid = r1.astype(BF16)
    lo = (r1 - mid.astype(F32)).astype(BF16)
    return _dot(a, hi) + _dot(a, mid) + _dot(a, lo)


def _silu(x):
    return x * jax.nn.sigmoid(x)


def _iota(shape, dim):
    return lax.broadcasted_iota(jnp.int32, shape, dim)


def _rows(i, n):
    return pl.ds(i * n if isinstance(i, int) else pl.multiple_of(i * n, n), n)


def _const_spec(shape):
    nd = len(shape)
    return pl.BlockSpec(shape, lambda *_: (0,) * nd)


def _params(sem):
    return pltpu.CompilerParams(dimension_semantics=sem, vmem_limit_bytes=VMEM_LIMIT)


def _inproj_kernel(x_ref, cs_ref, g_ref, wz_ref, wxbc_ref, wdt_ref, wq_ref, wk_ref, wv_ref,
                   convw_ref, convb_ref, dtb_ref, qg_ref, kg_ref,
                   z_out, xbc_out, dt_out, q_out, k_out, v_out, kf_out, vf_out, conv_out,
                   cbuf, hbuf, qbuf, kbuf, vbuf, *, tm, sub, keep):
    t = pl.program_id(1)
    last = pl.num_programs(1) - 1
    n_sub = tm // sub
    n_slab = CONV_DIM // LANES
    hist = SUBLANES - (D_CONV - 1)
    low = _iota((sub, LANES), 1) < HEAD_DIM

    @pl.when(t == 0)
    def _():
        for j in range(n_slab):
            cbuf[j, 0:SUBLANES, :] = jnp.zeros((SUBLANES, LANES), F32)
            cbuf[j, hist:SUBLANES, :] = cs_ref[0, :, j * LANES:(j + 1) * LANES]

    def norm(i):
        x = x_ref[0, _rows(i, sub), :]
        ms = jnp.mean(x * x, axis=-1, keepdims=True)
        hbuf[i % 2] = ((x * lax.rsqrt(ms + EPS)) * g_ref[...]).astype(BF16)

    def project(i):
        rows = _rows(i, sub)
        h = hbuf[i % 2]
        xbc = _dot(h, wxbc_ref[...])
        for j in range(n_slab):
            cbuf[j, pl.ds(SUBLANES + i * sub, sub), :] = xbc[:, j * LANES:(j + 1) * LANES]
        qbuf[i % 2] = _dot(h, wq_ref[...])
        kbuf[i % 2] = _dot(h, wk_ref[...])
        z_out[0, rows, :] = _dot(h, wz_ref[...]).astype(BF16)
        v = _dot(h, wv_ref[...])
        v_out[0, rows, :] = v.astype(BF16)
        vbuf[i % 2] = v
        dt_raw = _dot(h, wdt_ref[...]) + dtb_ref[...]
        dt_out[0, rows, :] = jnp.maximum(dt_raw, 0.0) + jnp.log1p(jnp.exp(-jnp.abs(dt_raw)))

    def head_norm(u, gain):
        out = []
        for pr in range(N_HEADS // 2):
            up = u[:, pr * LANES:(pr + 1) * LANES]
            sq = up * up
            ms_lo = jnp.sum(jnp.where(low, sq, 0.0), axis=-1, keepdims=True) * (1.0 / HEAD_DIM)
            ms_hi = jnp.sum(jnp.where(low, 0.0, sq), axis=-1, keepdims=True) * (1.0 / HEAD_DIM)
            out.append(up * lax.rsqrt(jnp.where(low, ms_lo, ms_hi) + EPS))
        return jnp.concatenate(out, axis=1) * gain

    def finish(i):
        rows = _rows(i, sub)
        for j in range(n_slab):
            sl = slice(j * LANES, (j + 1) * LANES)
            acc = convb_ref[:, sl]
            for tap in range(D_CONV):
                acc = acc + cbuf[j, pl.ds(hist + tap + i * sub, sub), :] * convw_ref[tap:tap + 1, sl]
            xbc_out[0, rows, sl] = _silu(acc).astype(BF16)
        qn = head_norm(qbuf[i % 2], qg_ref[...])
        q_out[0, rows, :] = (qn * (HEAD_DIM ** -0.5 * LOG2E)).astype(BF16)
        kn = head_norm(kbuf[i % 2], kg_ref[...])
        k_out[0, rows, :] = kn.astype(BF16)
        newest = pl.ds(max(i * sub - (tm - keep), 0), sub)
        kf_out[0, newest, :] = kn
        vf_out[0, newest, :] = vbuf[i % 2]

    norm(0)
    for i in range(n_sub):
        if i > 0:
            finish(i - 1)
        project(i)
        if i + 1 < n_sub:
            norm(i + 1)
    finish(n_sub - 1)

    @pl.when(t == last)
    def _():
        for j in range(n_slab):
            conv_out[0, :, j * LANES:(j + 1) * LANES] = cbuf[j, tm + hist:tm + SUBLANES, :]

    for j in range(n_slab):
        cbuf[j, 0:SUBLANES, :] = cbuf[j, tm:tm + SUBLANES, :]


def _in_proj(x, conv_state, p, *, tm, sub):
    bsz, seq, _ = x.shape
    keep = min(ATTN_PAST, seq)
    assert seq % tm == 0 and tm % sub == 0 and keep % sub == 0 and keep <= tm and seq >= D_CONV - 1
    tok = lambda w: pl.BlockSpec((1, tm, w), lambda b, t: (b, t, 0))
    per_b = lambda r, w: pl.BlockSpec((1, r, w), lambda b, t: (b, 0, 0))
    consts = [p["g_mix"], p["wz"], p["wxbc"], p["wdt"], p["wq"], p["wk"], p["wv"],
              p["conv_w"], p["conv_b"], p["dt_bias"], p["q_gain"], p["k_gain"]]
    out_shape = (
        jax.ShapeDtypeStruct((bsz, seq, SSD_WIDTH), BF16),
        jax.ShapeDtypeStruct((bsz, seq, CONV_DIM), BF16),
        jax.ShapeDtypeStruct((bsz, seq, LANES), F32),
        jax.ShapeDtypeStruct((bsz, seq, ATTN_WIDTH), BF16),
        jax.ShapeDtypeStruct((bsz, seq, ATTN_WIDTH), BF16),
        jax.ShapeDtypeStruct((bsz, seq, ATTN_WIDTH), BF16),
        jax.ShapeDtypeStruct((bsz, keep, ATTN_WIDTH), F32),
        jax.ShapeDtypeStruct((bsz, keep, ATTN_WIDTH), F32),
        jax.ShapeDtypeStruct((bsz, D_CONV - 1, CONV_DIM), F32),
    )
    out_specs = (tok(SSD_WIDTH), tok(CONV_DIM), tok(LANES), tok(ATTN_WIDTH), tok(ATTN_WIDTH),
                 tok(ATTN_WIDTH), per_b(keep, ATTN_WIDTH), per_b(keep, ATTN_WIDTH),
                 per_b(D_CONV - 1, CONV_DIM))
    return pl.pallas_call(
        functools.partial(_inproj_kernel, tm=tm, sub=sub, keep=keep),
        grid=(bsz, seq // tm),
        in_specs=[tok(D_MODEL), per_b(D_CONV - 1, CONV_DIM)] + [_const_spec(c.shape) for c in consts],
        out_specs=out_specs,
        out_shape=out_shape,
        scratch_shapes=[pltpu.VMEM((CONV_DIM // LANES, tm + SUBLANES, LANES), F32),
                        pltpu.VMEM((2, sub, D_MODEL), BF16),
                        pltpu.VMEM((2, sub, ATTN_WIDTH), F32),
                        pltpu.VMEM((2, sub, ATTN_WIDTH), F32),
                        pltpu.VMEM((2, sub, ATTN_WIDTH), F32)],
        compiler_params=_params(("arbitrary", "arbitrary")),
        name="in_proj",
    )(x, conv_state, *consts)


def _ssd_stages(xbc_ref, z_ref, dt_ref, alog_h_ref, dskip_ref, ng_ref, yo_out,
                state, sc_buf, xdt_buf, xend_buf, grow_buf, cdec_buf, *, q_len):
    n_pairs = N_HEADS // 2
    causal = _iota((q_len, q_len), 0) >= _iota((q_len, q_len), 1)
    tri = causal.astype(BF16)
    low = _iota((q_len, LANES), 1) < HEAD_DIM
    a_h = -jnp.exp(alog_h_ref[...]) * LOG2E
    zero = jnp.zeros((), BF16)

    def b_of(rows, g):
        return xbc_ref[0, rows, SSD_WIDTH + g * D_STATE:SSD_WIDTH + (g + 1) * D_STATE]

    def c_of(rows, g):
        return xbc_ref[0, rows, SSD_WIDTH + (N_GROUPS + g) * D_STATE:SSD_WIDTH + (N_GROUPS + g + 1) * D_STATE]

    def on_hp_lanes(cols):
        return jnp.concatenate(
            [jnp.where(low, cols[2 * i][:, 0:LANES], cols[2 * i + 1][:, 0:LANES]) for i in range(n_pairs)], axis=1)

    def local(c, slot):
        rows = _rows(c, q_len)
        dt = dt_ref[0, rows, :]
        cum_h = _dot_exact_lhs(tri, dt * a_h)
        cum_t = cum_h.T
        cum_b = [jnp.broadcast_to(cum_h[:, h:h + 1], (q_len, q_len)) for h in range(N_HEADS)]
        dt_b = [jnp.broadcast_to(dt[:, h:h + 1], (q_len, LANES)) for h in range(N_HEADS)]
        cum = on_hp_lanes(cum_b)
        cum_last = cum[q_len - 1:q_len, :]
        xdt = xbc_ref[0, rows, 0:SSD_WIDTH].astype(F32) * on_hp_lanes(dt_b)
        xdt_buf[slot] = xdt.astype(BF16)
        xend_buf[slot] = (xdt * jnp.exp2(cum_last - cum)).astype(BF16)
        grow_buf[slot] = jnp.exp2(cum)
        cdec_buf[slot] = jnp.exp2(cum_last)
        for g in range(N_GROUPS):
            cb = _dot_nt(c_of(rows, g), b_of(rows, g))
            for h in range(g * HEADS_PER_GROUP, (g + 1) * HEADS_PER_GROUP):
                seg = cum_b[h] - jnp.broadcast_to(cum_t[h:h + 1, :], (q_len, q_len))
                sc_buf[slot, h // 2, :, (h % 2) * q_len:(h % 2 + 1) * q_len] = (
                    cb * jnp.exp2(jnp.where(causal, seg, NEG))).astype(BF16)

    def carried(c, slot):
        rows = _rows(c, q_len)
        ys = []
        for g in range(N_GROUPS):
            lanes = slice(g * GROUP_WIDTH, (g + 1) * GROUP_WIDTH)
            st = state[:, lanes]
            y_diag = []
            for pr in range(g * HEADS_PER_GROUP // 2, (g + 1) * HEADS_PER_GROUP // 2):
                xp = xdt_buf[slot, :, pr * LANES:(pr + 1) * LANES]
                rhs = jnp.concatenate([jnp.where(low, xp, zero), jnp.where(low, zero, xp)], axis=0)
                y_diag.append(_dot(sc_buf[slot, pr], rhs))
            ys.append(jnp.concatenate(y_diag, axis=1)
                      + _dot(c_of(rows, g), st.astype(BF16)) * grow_buf[slot, :, lanes])
            state[:, lanes] = cdec_buf[slot, :, lanes] * st + _dot_tn(b_of(rows, g), xend_buf[slot, :, lanes])
        xs = xbc_ref[0, rows, 0:SSD_WIDTH].astype(F32)
        y = jnp.concatenate(ys, axis=1) + dskip_ref[...] * xs
        gated = y * _silu(z_ref[0, rows, :].astype(F32))
        ms = jnp.mean(gated * gated, axis=-1, keepdims=True)
        yo_out[0, rows, 0:SSD_WIDTH] = ((gated * lax.rsqrt(ms + EPS)) * ng_ref[...]).astype(BF16)

    return local, carried


def _attn_stages(q_ref, yo_out, kbuf, vbuf, bias, sbuf, mbuf, *, q_len, window):
    n_pairs = N_HEADS // 2
    low = _iota((q_len, LANES), 1) < HEAD_DIM
    col = _iota((2 * q_len, window), 1)
    zero = jnp.zeros((), BF16)
    ones = jnp.ones((window, LANES), BF16)

    def row0(c):
        return c * q_len if isinstance(c, int) else pl.multiple_of(c * q_len, q_len)

    def scores(c, slot, masked):
        r0 = row0(c)
        for pr in range(n_pairs):
            lanes = slice(pr * LANES, (pr + 1) * LANES)
            qp = q_ref[0, pl.ds(r0, q_len), lanes]
            lhs = jnp.concatenate([jnp.where(low, qp, zero), jnp.where(low, zero, qp)], axis=0)
            s = _dot_nt(lhs, kbuf[pl.ds(r0, window), lanes]) + bias[pr, :, 0:window]
            if masked:
                s = jnp.where(col >= ATTN_PAST - r0, s, NEG)
            sbuf[slot, pr] = s
            mbuf[slot, pr] = jnp.max(s, axis=-1, keepdims=True)

    def finish(c, slot):
        r0 = row0(c)
        for pr in range(n_pairs):
            lanes = slice(pr * LANES, (pr + 1) * LANES)
            e = jnp.exp2(sbuf[slot, pr] - mbuf[slot, pr]).astype(BF16)
            pv = _dot(e, jnp.concatenate([vbuf[pl.ds(r0, window), lanes], ones], axis=1))
            pv = pv[:, 0:LANES] / pv[:, LANES:]
            yo_out[0, pl.ds(r0, q_len), SSD_WIDTH + pr * LANES:SSD_WIDTH + (pr + 1) * LANES] = (
                jnp.where(low, pv[0:q_len], pv[q_len:]).astype(BF16))

    return scores, finish


def _mixer_kernel(xbc_ref, z_ref, dt_ref, h0_ref, alog_h_ref, dskip_ref, ng_ref,
                  q_ref, k_ref, v_ref, fk_ref, fv_ref, rel_ref,
                  yo_out, hfin_out,
                  state, sc_buf, xdt_buf, xend_buf, grow_buf, cdec_buf, kbuf, vbuf, bias, sbuf, mbuf,
                  *, s_len, q_len, n_blocks, n_masked):
    seq = k_ref.shape[1]
    window = ATTN_PAST + q_len
    per_block = min(s_len, seq) // q_len
    assert per_block in (1, 2)

    @pl.when(pl.program_id(0) == 0)
    def _():
        for h in range(N_HEADS):
            base = jnp.broadcast_to(rel_ref[h:h + 1, :], (q_len, BIAS_COLS))
            rows = pltpu.roll(base, BIAS_COLS - (CHUNK - 1), 1, stride=1, stride_axis=0)
            bias[h // 2, (h % 2) * q_len:(h % 2 + 1) * q_len, :] = rows * LOG2E

    state[...] = h0_ref[0].T
    if yo_out.shape[1] > seq:
        yo_out[0, seq:, SSD_WIDTH:] = jnp.zeros((yo_out.shape[1] - seq, ATTN_WIDTH), BF16)
    kbuf[0:ATTN_PAST, :] = fk_ref[0].astype(BF16)
    vbuf[0:ATTN_PAST, :] = fv_ref[0].astype(BF16)
    kbuf[ATTN_PAST:ATTN_PAST + seq, :] = k_ref[0]
    vbuf[ATTN_PAST:ATTN_PAST + seq, :] = v_ref[0]

    local, carried = _ssd_stages(xbc_ref, z_ref, dt_ref, alog_h_ref, dskip_ref, ng_ref, yo_out,
                                 state, sc_buf, xdt_buf, xend_buf, grow_buf, cdec_buf, q_len=s_len)
    scores, finish = _attn_stages(q_ref, yo_out, kbuf, vbuf, bias, sbuf, mbuf, q_len=q_len, window=window)

    def run_block(j, par, masked, has_next):
        c0 = per_block * j
        if has_next:
            local(j + 1, 1 - par)
        if per_block == 2:
            scores(c0 + 1, 1, masked)
            carried(j, par)
            finish(c0, 0)
            if has_next:
                scores(c0 + 2, 0, masked)
            finish(c0 + 1, 1)
        else:
            if has_next:
                scores(c0 + 1, 1 - par, masked)
            carried(j, par)
            finish(c0, par)

    local(0, 0)
    scores(0, 0, n_masked > 0)
    if n_blocks > 1:
        assert n_blocks % 2 == 0

        def two_blocks(masked):
            def body(i, carry):
                run_block(2 * i, 0, masked, True)
                run_block(2 * i + 1, 1, masked, True)
                return carry
            return body
        n_steps = n_blocks // 2 - 1
        masked_blocks = -(-n_masked // per_block)
        n_masked_steps = min(-(-masked_blocks // 2), n_steps)
        lax.fori_loop(0, n_masked_steps, two_blocks(True), 0)
        lax.fori_loop(n_masked_steps, n_steps, two_blocks(False), 0)
        tail_masked = n_masked > per_block * (n_blocks - 2)
        run_block(n_blocks - 2, 0, tail_masked, True)
        run_block(n_blocks - 1, 1, tail_masked, False)
    else:
        run_block(0, 0, n_masked > 0, False)
    hfin_out[0] = state[...].T


def _mixer(xbc, z, dt, h0, q, k, v, front_k, front_v, p, *, s_len, q_len, mask_front):
    bsz, seq, _ = q.shape
    rows = xbc.shape[1]
    n_blocks = rows // s_len
    assert rows % s_len == 0 and s_len % LANES == 0 and seq % q_len == 0 and rows >= seq
    assert front_k.shape[1] == ATTN_PAST and seq // q_len == n_blocks * (min(s_len, seq) // q_len)
    per_front = front_k.shape[0] == bsz
    scan = lambda w: pl.BlockSpec((1, rows, w), lambda b: (b, 0, 0))
    own = pl.BlockSpec((1, seq, ATTN_WIDTH), lambda b: (b, 0, 0))
    per_b = pl.BlockSpec((1, SSD_WIDTH, D_STATE), lambda b: (b, 0, 0))
    front = pl.BlockSpec((1, ATTN_PAST, ATTN_WIDTH),
                         (lambda b: (b, 0, 0)) if per_front else (lambda b: (0, 0, 0)))
    consts = [p["a_log_h"], p["d_skip_p"], p["ssd_gain"]]
    window = ATTN_PAST + q_len
    return pl.pallas_call(
        functools.partial(_mixer_kernel, s_len=s_len, q_len=q_len, n_blocks=n_blocks,
                          n_masked=min(BAND_CHUNKS, seq // q_len) if mask_front else 0),
        grid=(bsz,),
        in_specs=([scan(CONV_DIM), scan(SSD_WIDTH), scan(LANES), per_b] + [_const_spec(c.shape) for c in consts]
                  + [own, own, own, front, front, _const_spec(p["rel_rows"].shape)]),
        out_specs=(scan(D_MODEL), per_b),
        out_shape=(jax.ShapeDtypeStruct((bsz, rows, D_MODEL), BF16),
                   jax.ShapeDtypeStruct((bsz, SSD_WIDTH, D_STATE), F32)),
        scratch_shapes=[pltpu.VMEM((D_STATE, SSD_WIDTH), F32),
                        pltpu.VMEM((2, N_HEADS // 2, s_len, 2 * s_len), BF16),
                        pltpu.VMEM((2, s_len, SSD_WIDTH), BF16),
                        pltpu.VMEM((2, s_len, SSD_WIDTH), BF16),
                        pltpu.VMEM((2, s_len, SSD_WIDTH), F32),
                        pltpu.VMEM((2, 1, SSD_WIDTH), F32),
                        pltpu.VMEM((ATTN_PAST + seq, ATTN_WIDTH), BF16),
                        pltpu.VMEM((ATTN_PAST + seq, ATTN_WIDTH), BF16),
                        pltpu.VMEM((N_HEADS // 2, 2 * q_len, BIAS_COLS), F32),
                        pltpu.VMEM((2, N_HEADS // 2, 2 * q_len, window), F32),
                        pltpu.VMEM((2, N_HEADS // 2, 2 * q_len, 1), F32)],
        compiler_params=_params(("arbitrary",)),
        name="mixer",
    )(xbc, z, dt, h0, *consts, q, k, v, front_k, front_v, p["rel_rows"])


def _outffn_kernel(x_ref, yo_ref, wo_ref, g_ref, wg_ref, wu_ref, wd_ref, out_ref, act):
    x1 = x_ref[...] + _dot(yo_ref[...], wo_ref[...])
    out_ref[...] = x1
    ms = jnp.mean(x1 * x1, axis=-1, keepdims=True)
    f = ((x1 * lax.rsqrt(ms + EPS)) * g_ref[...]).astype(BF16)
    for start, width in FF_CHUNKS:
        gate = _dot(f, wg_ref[:, start:start + width])
        up = _dot(f, wu_ref[:, start:start + width])
        act[:, start:start + width] = (_silu(gate) * up).astype(BF16)
    out_ref[...] += _dot(act[...], wd_ref[...])


def _out_ffn(x, yo, p, *, tm):
    n, _ = x.shape
    assert n % tm == 0
    tok = lambda w: pl.BlockSpec((tm, w), lambda i: (i, 0))
    consts = [p["w_out"], p["g_ffn"], p["w_gate"], p["w_up"], p["w_down"]]
    single = lambda c: pl.BlockSpec(c.shape, lambda i: (0,) * c.ndim, pipeline_mode=pl.Buffered(1))
    return pl.pallas_call(
        _outffn_kernel,
        grid=(n // tm,),
        in_specs=[tok(D_MODEL), tok(D_MODEL)] + [single(c) for c in consts],
        out_specs=tok(D_MODEL),
        out_shape=jax.ShapeDtypeStruct((n, D_MODEL), F32),
        scratch_shapes=[pltpu.VMEM((tm, D_FF), BF16)],
        compiler_params=_params(("arbitrary",)),
        name="out_ffn",
    )(x, yo, *consts)


def _prep_layer(norm_mix_g, w_in, conv_w, conv_b, dt_bias, a_log, d_skip, ssd_norm_g,
                q_norm_g, k_norm_g, rel_bias, w_out, norm_ffn_g, w_gate, w_up, w_down):
    o_z, o_xbc, o_dt = 0, SSD_WIDTH, SSD_WIDTH + CONV_DIM
    o_q = o_dt + N_HEADS
    o_k, o_v = o_q + ATTN_WIDTH, o_q + 2 * ATTN_WIDTH
    wb = w_in.astype(BF16)
    wdt = jnp.zeros((D_MODEL, LANES), BF16).at[:, :N_HEADS].set(wb[:, o_dt:o_q])
    rel_idx = jnp.clip(ATTN_PAST + CHUNK - 1 - jnp.arange(BIAS_COLS), -MAX_REL, MAX_REL) + MAX_REL
    return dict(
        g_mix=norm_mix_g[None, :],
        wz=wb[:, o_z:o_xbc], wxbc=wb[:, o_xbc:o_dt], wdt=wdt,
        wq=wb[:, o_q:o_k], wk=wb[:, o_k:o_v], wv=wb[:, o_v:],
        conv_w=conv_w, conv_b=conv_b[None, :],
        dt_bias=jnp.zeros((1, LANES), F32).at[0, :N_HEADS].set(dt_bias),
        q_gain=jnp.tile(q_norm_g, N_HEADS)[None, :], k_gain=jnp.tile(k_norm_g, N_HEADS)[None, :],
        a_log_h=jnp.zeros((1, LANES), F32).at[0, :N_HEADS].set(a_log),
        d_skip_p=jnp.repeat(d_skip, HEAD_DIM)[None, :], ssd_gain=ssd_norm_g[None, :],
        rel_rows=rel_bias[:, rel_idx],
        w_out=w_out.astype(BF16), g_ffn=norm_ffn_g[None, :],
        w_gate=w_gate.astype(BF16), w_up=w_up.astype(BF16), w_down=w_down.astype(BF16),
    )


def _layer(x, conv_state, ssm_state, cache_k, cache_v, p):
    bsz, seq, _ = x.shape
    z, xbc, dt, q, k, v, k_new, v_new, conv_new = _in_proj(x, conv_state, p, tm=min(INPROJ_TILE, seq),
                                                           sub=min(INPROJ_SUB, seq))
    h0 = ssm_state.reshape(bsz, SSD_WIDTH, D_STATE)
    pad = (-seq) % SSD_CHUNK
    pad_rows = lambda a: jnp.pad(a, ((0, 0), (0, pad), (0, 0))) if pad else a
    if cache_k is None:
        front_k = front_v = jnp.zeros((1, ATTN_PAST, ATTN_WIDTH), F32)
        q_len = CHUNK
    else:
        front_k = cache_k.reshape(bsz, ATTN_PAST, ATTN_WIDTH)
        front_v = cache_v.reshape(bsz, ATTN_PAST, ATTN_WIDTH)
        q_len = seq
    yo, h_fin = _mixer(pad_rows(xbc), pad_rows(z), pad_rows(dt), h0, q, k, v, front_k, front_v, p,
                       s_len=SSD_CHUNK, q_len=q_len, mask_front=cache_k is None)
    n = bsz * seq
    out = _out_ffn(x.reshape(n, D_MODEL), yo[:, :seq].reshape(n, D_MODEL), p, tm=min(FFN_TILE, n))
    keep = k_new.shape[1]
    return (out.reshape(bsz, seq, D_MODEL),
            k_new.reshape(bsz, keep, N_HEADS, HEAD_DIM), v_new.reshape(bsz, keep, N_HEADS, HEAD_DIM),
            h_fin.reshape(bsz, N_HEADS, HEAD_DIM, D_STATE), conv_new)


def kernel(x_prompt, x_sample, cache_attn_k, cache_attn_v, state_ssm, state_conv, norm_mix_g, w_in, conv_w, conv_b, dt_bias, a_log, d_skip, ssd_norm_g, q_norm_g, k_norm_g, rel_bias, w_out, norm_ffn_g, w_gate, w_up, w_down):
    weights = [norm_mix_g, w_in, conv_w, conv_b, dt_bias, a_log, d_skip, ssd_norm_g,
               q_norm_g, k_norm_g, rel_bias, w_out, norm_ffn_g, w_gate, w_up, w_down]
    depth = w_in.shape[0]
    assert cache_attn_k.shape[2] == ATTN_PAST
    yp, ys = x_prompt, x_sample
    outs = [[] for _ in range(8)]
    for i in range(depth):
        p = _prep_layer(*[w[i] for w in weights])
        bp = yp.shape[0]
        zero_conv = jnp.zeros((bp, D_CONV - 1, CONV_DIM), F32)
        zero_ssm = jnp.zeros((bp, N_HEADS, HEAD_DIM, D_STATE), F32)
        yp, kp, vp, sp, cp = _layer(yp, zero_conv, zero_ssm, None, None, p)
        ys, kn, vn, sn, cn = _layer(ys, state_conv[i], state_ssm[i], cache_attn_k[i], cache_attn_v[i], p)
        for lst, val in zip(outs, (kp, vp, sp, cp, kn, vn, sn, cn)):
            lst.append(val)
    return (yp, ys) + tuple(jnp.stack(lst) for lst in outs)
```
